```python
import math
import jax, jax.numpy as jnp
from jax import lax
import numpy as np

D_MODEL = 1024
BATCH = 2
SEQ = 16384
DEPTH = 4

N_A = DEPTH // 2
N_B = DEPTH - N_A
CONV_WIDTH = 31
N_HEADS = 16
N_KV_HEADS = 2
HEAD_DIM = 64
GROUP = N_HEADS // N_KV_HEADS
WINDOW = 128
ATT_BLK = 128
NUM_BUCKETS = 32
MAX_DISTANCE = 128
N_EXPERTS = 32
TOP_K = 4
D_FF = D_MODEL
SWIGLU_LIMIT = 7.0
SWIGLU_ALPHA = 1.702
MOE_BLK = 128
EPS = 1e-5
NEG = -1e30

kernel_name = "yoco_conformer_swa_sink_moe"


def rmsnorm(x, g):
    xf = x.astype(jnp.float32)
    y = xf * lax.rsqrt(jnp.mean(xf * xf, axis=-1, keepdims=True) + EPS)
    return (y * g.astype(jnp.float32)).astype(x.dtype)


def layernorm(x, g, b):
    xf = x.astype(jnp.float32)
    mu = jnp.mean(xf, axis=-1, keepdims=True)
    var = jnp.mean(jnp.square(xf - mu), axis=-1, keepdims=True)
    y = (xf - mu) * lax.rsqrt(var + EPS)
    return (y * g.astype(jnp.float32) + b.astype(jnp.float32)).astype(x.dtype)


def conformer_conv(h, pw1_w, pw1_b, dw_w, dw_b, ln_g, ln_b, pw2_w, pw2_b):
    u = h @ pw1_w + pw1_b
    a, gate = jnp.split(u, 2, axis=-1)
    u = a * jax.nn.sigmoid(gate)
    u = lax.conv_general_dilated(
        u, dw_w[:, None, :].astype(u.dtype), window_strides=(1,),
        padding=[(CONV_WIDTH - 1, 0)],
        dimension_numbers=("NWC", "WIO", "NWC"),
        feature_group_count=D_MODEL) + dw_b
    u = layernorm(u, ln_g, ln_b)
    u = jax.nn.silu(u)
    return u @ pw2_w + pw2_b


def t5_bucket(n):
    max_exact = NUM_BUCKETS // 2
    nf = jnp.maximum(n, max_exact).astype(jnp.float32)
    large = max_exact + (jnp.log(nf / max_exact) / math.log(MAX_DISTANCE / max_exact)
                         * (NUM_BUCKETS - max_exact)).astype(jnp.int32)
    large = jnp.minimum(large, NUM_BUCKETS - 1)
    return jnp.where(n < max_exact, n, large)


def to_band(t):
    b, s = t.shape[0], t.shape[1]
    nb = s // ATT_BLK
    tp = jnp.pad(t, ((0, 0), (ATT_BLK, 0), (0, 0), (0, 0)))
    prev = tp[:, :s].reshape(b, nb, ATT_BLK, N_KV_HEADS, HEAD_DIM)
    cur = t.reshape(b, nb, ATT_BLK, N_KV_HEADS, HEAD_DIM)
    return jnp.concatenate([prev, cur], axis=2)


def band_bias_mask(rel_bias, nb):
    qi = jnp.arange(ATT_BLK, dtype=jnp.int32)[:, None]
    kj = jnp.arange(2 * ATT_BLK, dtype=jnp.int32)[None, :]
    dist = qi + ATT_BLK - kj
    in_win = (dist >= 0) & (dist < WINDOW)
    blk = jnp.arange(nb, dtype=jnp.int32)[:, None, None]
    valid = in_win[None] & ((blk > 0) | (kj >= ATT_BLK)[None])
    bucket = t5_bucket(jnp.maximum(dist, 0))
    bias = jnp.transpose(rel_bias[bucket], (2, 0, 1))
    bias = bias.reshape(N_KV_HEADS, GROUP, ATT_BLK, 2 * ATT_BLK).astype(jnp.float32)
    return bias, valid


def swa_sink_attention(h, k_band, v_band, bias, valid, w_q, b_q, sinks, w_o, b_o):
    b, s = h.shape[0], h.shape[1]
    nb = s // ATT_BLK
    q = (h @ w_q + b_q).reshape(b, nb, ATT_BLK, N_KV_HEADS, GROUP, HEAD_DIM)
    scores = jnp.einsum("bnqhgd,bnkhd->bnhgqk", q, k_band).astype(jnp.float32)
    scores = scores * (HEAD_DIM ** -0.5) + bias[None, None]
    scores = jnp.where(valid[None, :, None, None], scores, NEG)
    sink = sinks.astype(jnp.float32).reshape(N_KV_HEADS, GROUP)[None, None, :, :, None, None]
    m = jnp.maximum(jnp.max(scores, axis=-1, keepdims=True), sink)
    p = jnp.exp(scores - m)
    denom = jnp.sum(p, axis=-1, keepdims=True) + jnp.exp(sink - m)
    probs = (p / denom).astype(v_band.dtype)
    o = jnp.einsum("bnhgqk,bnkhd->bnqhgd", probs, v_band).reshape(b, s, N_HEADS * HEAD_DIM)
    return o @ w_o + b_o


def moe_ffn(h, router_w, router_b, w1, b1, w2, b2):
    t = h.shape[0]
    logits = (h @ router_w + router_b).astype(jnp.float32)
    top_v, top_i = lax.top_k(logits, TOP_K)
    gates = jax.nn.softmax(top_v, axis=-1).astype(h.dtype)
    n_assign = t * TOP_K
    flat_e = top_i.reshape(n_assign).astype(jnp.int32)
    flat_t = jnp.arange(n_assign, dtype=jnp.int32) // TOP_K
    flat_g = gates.reshape(n_assign)
    order = jnp.argsort(flat_e)
    se = flat_e[order]
    counts = jnp.zeros((N_EXPERTS,), jnp.int32).at[flat_e].add(1)
    padded = (counts + MOE_BLK - 1) // MOE_BLK * MOE_BLK
    pad_end = jnp.cumsum(padded)
    pad_start = pad_end - padded
    start = jnp.cumsum(counts) - counts
    dest = pad_start[se] + jnp.arange(n_assign, dtype=jnp.int32) - start[se]
    cap = n_assign + N_EXPERTS * MOE_BLK
    n_blk = cap // MOE_BLK
    row_tok = jnp.zeros((cap,), jnp.int32).at[dest].set(flat_t[order])
    row_gate = jnp.zeros((cap,), h.dtype).at[dest].set(flat_g[order])
    blk_e = jnp.minimum(jnp.searchsorted(pad_end, jnp.arange(n_blk, dtype=jnp.int32) * MOE_BLK,
                                         side="right"), N_EXPERTS - 1).astype(jnp.int32)
    xs = h[row_tok].reshape(n_blk, MOE_BLK, D_MODEL)

    def expert_block(args):
        xb, e = args
        u = xb @ w1[e] + b1[e]
        g, lin = u[:, :D_FF], u[:, D_FF:]
        g = jnp.minimum(g, SWIGLU_LIMIT)
        lin = jnp.clip(lin, -SWIGLU_LIMIT, SWIGLU_LIMIT)
        a = g * jax.nn.sigmoid(SWIGLU_ALPHA * g) * (lin + 1)
        return a @ w2[e] + b2[e]

    ys = lax.map(expert_block, (xs, blk_e)).reshape(cap, D_MODEL)
    return jnp.zeros_like(h).at[row_tok].add(ys * row_gate[:, None])


def setup_inputs(seed: int = 0) -> dict:
    key = jax.random.key(seed)
    ks = iter(jax.random.split(key, 40))
    f32 = jnp.float32

    def nrm(shape, scale):
        return jax.random.normal(next(ks), shape, f32) * scale

    def gain(shape):
        return 1.0 + nrm(shape, 0.02)

    d, kvw = D_MODEL, 2 * N_KV_HEADS * HEAD_DIM
    return {
        "x": nrm((BATCH, SEQ, d), 1.0),
        "conv_norm_g": gain((N_A, d)),
        "conv_pw1_w": nrm((N_A, d, 2 * d), d ** -0.5),
        "conv_pw1_b": nrm((N_A, 2 * d), 0.02),
        "conv_dw_w": nrm((N_A, CONV_WIDTH, d), CONV_WIDTH ** -0.5),
        "conv_dw_b": nrm((N_A, d), 0.02),
        "conv_ln_g": gain((N_A, d)),
        "conv_ln_b": nrm((N_A, d), 0.02),
        "conv_pw2_w": nrm((N_A, d, d), 0.5 * d ** -0.5),
        "conv_pw2_b": nrm((N_A, d), 0.02),
        "kv_norm_g": gain((d,)),
        "w_kv": nrm((d, kvw), d ** -0.5),
        "b_kv": nrm((kvw,), 0.02),
        "attn_norm_g": gain((N_B, d)),
        "w_q": nrm((N_B, d, N_HEADS * HEAD_DIM), d ** -0.5),
        "b_q": nrm((N_B, N_HEADS * HEAD_DIM), 0.02),
        "sinks": nrm((N_B, N_HEADS), 0.5),
        "w_o": nrm((N_B, N_HEADS * HEAD_DIM, d), 0.5 * (N_HEADS * HEAD_DIM) ** -0.5),
        "b_o": nrm((N_B, d), 0.02),
        "rel_bias": nrm((NUM_BUCKETS, N_HEADS), 0.2),
        "moe_norm_g": gain((DEPTH, d)),
        "router_w": nrm((DEPTH, d, N_EXPERTS), d ** -0.5),
        "router_b": nrm((DEPTH, N_EXPERTS), 0.01),
        "moe_w1": nrm((DEPTH, N_EXPERTS, d, 2 * D_FF), d ** -0.5),
        "moe_b1": nrm((DEPTH, N_EXPERTS, 2 * D_FF), 0.02),
        "moe_w2": nrm((DEPTH, N_EXPERTS, D_FF, d), 0.5 * D_FF ** -0.5),
        "moe_b2": nrm((DEPTH, N_EXPERTS, d), 0.02),
        "final_norm_g": gain((d,)),
    }


def reference(x, conv_norm_g, conv_pw1_w, conv_pw1_b, conv_dw_w, conv_dw_b, conv_ln_g, conv_ln_b,
              conv_pw2_w, conv_pw2_b, kv_norm_g, w_kv, b_kv, attn_norm_g, w_q, b_q, sinks, w_o, b_o,
              rel_bias, moe_norm_g, router_w, router_b, moe_w1, moe_b1, moe_w2, moe_b2, final_norm_g):
    b, s, d = x.shape
    nb = s // ATT_BLK
    bias, valid = band_bias_mask(rel_bias, nb)
    k_band = v_band = None
    for l in range(DEPTH):
        if l < N_A:
            x = x + conformer_conv(rmsnorm(x, conv_norm_g[l]), conv_pw1_w[l], conv_pw1_b[l],
                                   conv_dw_w[l], conv_dw_b[l], conv_ln_g[l], conv_ln_b[l],
                                   conv_pw2_w[l], conv_pw2_b[l])
        else:
            j = l - N_A
            x = x + swa_sink_attention(rmsnorm(x, attn_norm_g[j]), k_band, v_band, bias, valid,
                                       w_q[j], b_q[j], sinks[j], w_o[j], b_o[j])
        h = rmsnorm(x, moe_norm_g[l]).reshape(b * s, d)
        x = x + moe_ffn(h, router_w[l], router_b[l], moe_w1[l], moe_b1[l],
                        moe_w2[l], moe_b2[l]).reshape(b, s, d)
        if l == N_A - 1:
            kv = rmsnorm(x, kv_norm_g) @ w_kv + b_kv
            k, v = jnp.split(kv.reshape(b, s, 2 * N_KV_HEADS, HEAD_DIM), 2, axis=2)
            k_band, v_band = to_band(k), to_band(v)
    return rmsnorm(x, final_norm_g)
```

```python
import functools
import math

import numpy as np
import jax
import jax.numpy as jnp
from jax import lax
from jax.experimental import pallas as pl
from jax.experimental.pallas import tpu as pltpu

F32 = jnp.float32
BF16 = jnp.bfloat16
I32 = jnp.int32

CONV_WIDTH = 31
N_HEADS = 16
N_KV_HEADS = 2
HEAD_DIM = 64
GROUP = N_HEADS // N_KV_HEADS
WINDOW = 128
ATT_BLK = 128
NUM_BUCKETS = 32
MAX_DISTANCE = 128
N_EXPERTS = 32
TOP_K = 4
SWIGLU_LIMIT = 7.0
SWIGLU_ALPHA = 1.702
EPS = 1e-5
NEG = -1e30

V7X_VMEM_BYTES = 64 * 1024 * 1024
VMEM_LIMIT_BYTES = V7X_VMEM_BYTES - 8 * 1024 * 1024
SUBLANES = 8
LANES = 128

CONV_TM = 256
CONV_HALO = 32
CONV_RC = 64
ROUTE_TM = 512
DISPATCH_TM = 256
COMBINE_TM = 128
EXPERT_BM = 256
KV_TM = 512


def _cparams(semantics):
    return pltpu.CompilerParams(dimension_semantics=semantics, vmem_limit_bytes=VMEM_LIMIT_BYTES)


def _rmsnorm(x, g):
    return x * lax.rsqrt(jnp.mean(x * x, axis=-1, keepdims=True) + EPS) * g


def _sigmoid(x):
    return 1.0 / (1.0 + jnp.exp(-x))


def _conv_mixer_kernel(x_ref, ng_ref, w1_ref, b1_ref, dw_ref, dwb_ref, lng_ref, lnb_ref, w2_ref, b2_ref,
                       o_ref, win_ref, conv_ref):
    tm, d = x_ref.shape[1], x_ref.shape[2]
    halo = win_ref.shape[0] - tm

    @pl.when(pl.program_id(1) == 0)
    def _():
        win_ref[0:halo, :] = jnp.zeros((halo, d), F32)

    x = x_ref[0]
    hn = _rmsnorm(x, ng_ref[...]).astype(BF16)
    u = jnp.dot(hn, w1_ref[...], preferred_element_type=F32) + b1_ref[...]
    win_ref[halo:halo + tm, :] = u[:, :d] * _sigmoid(u[:, d:])

    base = halo - (CONV_WIDTH - 1)
    wrows = CONV_RC + halo
    for c in range(d // LANES):
        cs = slice(c * LANES, (c + 1) * LANES)
        taps = [dw_ref[k:k + 1, cs] for k in range(CONV_WIDTH)]

        def chunk(r, carry, cs=cs, taps=taps):
            r0 = pl.multiple_of(r * CONV_RC, CONV_RC)
            w = win_ref[pl.ds(r0, wrows), cs]
            acc = None
            for b in range(SUBLANES):
                wb = w if b == 0 else pltpu.roll(w, wrows - b, 0)
                for a in range((base + CONV_WIDTH - 1 - b) // SUBLANES + 1):
                    k = SUBLANES * a + b - base
                    if 0 <= k < CONV_WIDTH:
                        term = taps[k] * wb[SUBLANES * a:SUBLANES * a + CONV_RC]
                        acc = term if acc is None else acc + term
            conv_ref[pl.ds(r0, CONV_RC), cs] = acc
            return carry

        lax.fori_loop(0, tm // CONV_RC, chunk, 0)

    win_ref[0:halo, :] = win_ref[tm:tm + halo, :]

    v = conv_ref[...] + dwb_ref[...]
    mu = jnp.mean(v, axis=-1, keepdims=True)
    vc = v - mu
    var = jnp.mean(vc * vc, axis=-1, keepdims=True)
    y = vc * lax.rsqrt(var + EPS) * lng_ref[...] + lnb_ref[...]
    y = (y * _sigmoid(y)).astype(BF16)
    o_ref[0] = x + jnp.dot(y, w2_ref[...], preferred_element_type=F32) + b2_ref[...]


def conv_mixer(x, norm_g, pw1_w, pw1_b, dw_w, dw_b, ln_g, ln_b, pw2_w, pw2_b, *, tm=CONV_TM):
    b, s, d = x.shape
    tm = min(tm, s)
    row = lambda v: v.reshape(1, -1).astype(F32)
    const2 = lambda shape: pl.BlockSpec(shape, lambda i, j: (0, 0))
    return pl.pallas_call(
        _conv_mixer_kernel,
        grid=(b, s // tm),
        in_specs=[
            pl.BlockSpec((1, tm, d), lambda i, j: (i, j, 0)),
            const2((1, d)), const2((d, 2 * d)), const2((1, 2 * d)),
            const2((CONV_WIDTH, d)), const2((1, d)), const2((1, d)), const2((1, d)),
            const2((d, d)), const2((1, d)),
        ],
        out_specs=pl.BlockSpec((1, tm, d), lambda i, j: (i, j, 0)),
        out_shape=jax.ShapeDtypeStruct((b, s, d), F32),
        scratch_shapes=[pltpu.VMEM((tm + CONV_HALO, d), F32), pltpu.VMEM((tm, d), F32)],
        compiler_params=_cparams(("arbitrary", "arbitrary")),
        name="conv_mixer",
    )(x, row(norm_g), pw1_w.astype(BF16), row(pw1_b), dw_w.astype(F32), row(dw_b), row(ln_g), row(ln_b),
      pw2_w.astype(BF16), row(pw2_b))


def _kv_kernel(x_ref, g_ref, w_ref, b_ref, k_ref, v_ref):
    hn = _rmsnorm(x_ref[...], g_ref[...]).astype(BF16)
    kv = jnp.dot(hn, w_ref[...], preferred_element_type=F32) + b_ref[...]
    half = kv.shape[1] // 2
    k_ref[...] = kv[:, :half].astype(BF16)
    v_ref[...] = kv[:, half:].astype(BF16)


def kv_project(x2, g, w_kv, b_kv, *, tm=KV_TM):
    t, d = x2.shape
    tm = min(tm, t)
    kvw = w_kv.shape[1]
    half = kvw // 2
    return pl.pallas_call(
        _kv_kernel,
        grid=(t // tm,),
        in_specs=[pl.BlockSpec((tm, d), lambda i: (i, 0)), pl.BlockSpec((1, d), lambda i: (0, 0)),
                  pl.BlockSpec((d, kvw), lambda i: (0, 0)), pl.BlockSpec((1, kvw), lambda i: (0, 0))],
        out_specs=[pl.BlockSpec((tm, half), lambda i: (i, 0)), pl.BlockSpec((tm, half), lambda i: (i, 0))],
        out_shape=[jax.ShapeDtypeStruct((t, half), BF16), jax.ShapeDtypeStruct((t, half), BF16)],
        compiler_params=_cparams(("arbitrary",)),
        name="kv_project",
    )(x2, g.reshape(1, d), w_kv.astype(BF16), b_kv.reshape(1, kvw))


def _bucket_table():
    qi = np.arange(ATT_BLK)[:, None]
    kj = np.arange(2 * ATT_BLK)[None, :]
    n = np.maximum(qi + ATT_BLK - kj, 0)
    max_exact = NUM_BUCKETS // 2
    nf = np.maximum(n, max_exact).astype(np.float32)
    large = max_exact + (np.log(nf / np.float32(max_exact)) / np.float32(math.log(MAX_DISTANCE / max_exact))
                         * np.float32(NUM_BUCKETS - max_exact)).astype(np.int32)
    large = np.minimum(large, NUM_BUCKETS - 1)
    return np.where(n < max_exact, n, large).astype(np.int32)


def _bias_kernel(rb_ref, bucket_ref, o_ref):
    h = pl.program_id(0)
    bucket = bucket_ref[...]
    acc = jnp.zeros(bucket.shape, F32)
    for b in range(NUM_BUCKETS):
        acc = jnp.where(bucket == b, rb_ref[b, h], acc)
    o_ref[0] = acc


def bias_table(rel_bias):
    bucket = jnp.asarray(_bucket_table())
    return pl.pallas_call(
        _bias_kernel,
        grid=(N_HEADS,),
        in_specs=[pl.BlockSpec(memory_space=pltpu.SMEM),
                  pl.BlockSpec((ATT_BLK, 2 * ATT_BLK), lambda h: (0, 0))],
        out_specs=pl.BlockSpec((1, ATT_BLK, 2 * ATT_BLK), lambda h: (h, 0, 0)),
        out_shape=jax.ShapeDtypeStruct((N_HEADS, ATT_BLK, 2 * ATT_BLK), F32),
        compiler_params=_cparams(("arbitrary",)),
        name="bias_table",
    )(rel_bias.astype(F32), bucket)


def _attn_kernel(sink_ref, x_ref, g_ref, wq_ref, bq_ref, kp_ref, kc_ref, vp_ref, vc_ref, bias_ref,
                 wo_ref, bo_ref, o_ref):
    n = pl.program_id(1)
    blk = x_ref.shape[1]
    x = x_ref[0]
    hn = _rmsnorm(x, g_ref[...]).astype(BF16)
    q = ((jnp.dot(hn, wq_ref[...], preferred_element_type=F32) + bq_ref[...]) * (HEAD_DIM ** -0.5)).astype(BF16)
    k_band = jnp.concatenate([kp_ref[0], kc_ref[0]], axis=0)
    v_band = jnp.concatenate([vp_ref[0], vc_ref[0]], axis=0)

    qi = lax.broadcasted_iota(I32, (blk, 2 * blk), 0)
    kj = lax.broadcasted_iota(I32, (blk, 2 * blk), 1)
    dist = qi + blk - kj
    valid = (dist >= 0) & (dist < WINDOW) & ((n > 0) | (kj >= blk))

    outs = []
    for h in range(N_HEADS):
        g = h // GROUP
        qh = q[:, h * HEAD_DIM:(h + 1) * HEAD_DIM]
        kh = k_band[:, g * HEAD_DIM:(g + 1) * HEAD_DIM]
        vh = v_band[:, g * HEAD_DIM:(g + 1) * HEAD_DIM]
        s = lax.dot_general(qh, kh, (((1,), (1,)), ((), ())), preferred_element_type=F32)
        s = jnp.where(valid, s + bias_ref[h], NEG)
        sink = sink_ref[h]
        m = jnp.maximum(jnp.max(s, axis=-1, keepdims=True), sink)
        p = jnp.exp(s - m)
        denom = jnp.sum(p, axis=-1, keepdims=True) + jnp.exp(sink - m)
        probs = (p / denom).astype(BF16)
        outs.append(jnp.dot(probs, vh, preferred_element_type=F32))
    o = jnp.concatenate(outs, axis=-1).astype(BF16)
    o_ref[0] = x + jnp.dot(o, wo_ref[...], preferred_element_type=F32) + bo_ref[...]


def swa_attention(x, k2, v2, bias, norm_g, w_q, b_q, sinks, w_o, b_o):
    b, s, d = x.shape
    nb = s // ATT_BLK
    kvw = k2.shape[-1]
    k3 = k2.reshape(b, s, kvw)
    v3 = v2.reshape(b, s, kvw)
    hd = w_q.shape[1]
    const = lambda shape: pl.BlockSpec(shape, lambda i, j, *_: tuple(0 for _ in shape))
    prev = pl.BlockSpec((1, ATT_BLK, kvw), lambda i, j, *_: (i, jnp.maximum(j - 1, 0), 0))
    cur = pl.BlockSpec((1, ATT_BLK, kvw), lambda i, j, *_: (i, j, 0))
    return pl.pallas_call(
        _attn_kernel,
        grid_spec=pltpu.PrefetchScalarGridSpec(
            num_scalar_prefetch=1,
            grid=(b, nb),
            in_specs=[
                pl.BlockSpec((1, ATT_BLK, d), lambda i, j, *_: (i, j, 0)),
                const((1, d)), const((d, hd)), const((1, hd)),
                prev, cur, prev, cur,
                const((N_HEADS, ATT_BLK, 2 * ATT_BLK)),
                const((hd, d)), const((1, d)),
            ],
            out_specs=pl.BlockSpec((1, ATT_BLK, d), lambda i, j, *_: (i, j, 0)),
        ),
        out_shape=jax.ShapeDtypeStruct((b, s, d), F32),
        compiler_params=_cparams(("arbitrary", "arbitrary")),
        name="swa_attention",
    )(sinks.astype(F32), x, norm_g.reshape(1, d), w_q.astype(BF16), b_q.reshape(1, hd), k3, k3, v3, v3, bias,
      w_o.astype(BF16), b_o.reshape(1, d))


def _route_kernel(x_ref, g_ref, rw_ref, rb_ref, tri_ref, h_ref, ti_ref, gate_ref, rank_ref, cnt_ref, run_ref):
    tm = x_ref.shape[0]
    ne = rw_ref.shape[0]

    @pl.when(pl.program_id(0) == 0)
    def _():
        run_ref[...] = jnp.zeros(run_ref.shape, F32)

    h = _rmsnorm(x_ref[...], g_ref[...])
    h_ref[...] = h
    logits = lax.dot_general(rw_ref[...], h, (((1,), (1,)), ((), ())), preferred_element_type=F32,
                             precision=lax.Precision.HIGHEST) + rb_ref[...]
    iota_e = lax.broadcasted_iota(I32, (ne, tm), 0)
    work = logits
    vals, idxs, hots = [], [], []
    for _ in range(TOP_K):
        m = jnp.max(work, axis=0, keepdims=True)
        idx = jnp.min(jnp.where(work == m, iota_e, ne), axis=0, keepdims=True)
        hot = iota_e == idx
        work = jnp.where(hot, -jnp.inf, work)
        vals.append(m)
        idxs.append(idx)
        hots.append(hot)
    exps = [jnp.exp(v - vals[0]) for v in vals]
    denom = exps[0] + exps[1] + exps[2] + exps[3]
    gate_ref[...] = jnp.concatenate([e / denom for e in exps], axis=0)
    ti_ref[...] = jnp.concatenate(idxs, axis=0)

    multi = jnp.zeros((ne, tm), F32)
    for hot in hots:
        multi = multi + hot.astype(F32)
    before = jnp.dot(multi.astype(BF16), tri_ref[...], preferred_element_type=F32) + run_ref[:, 0:1]
    ranks = [jnp.sum(jnp.where(hot, before, 0.0), axis=0, keepdims=True) for hot in hots]
    rank_ref[...] = jnp.concatenate(ranks, axis=0).astype(I32)
    run = run_ref[...] + jnp.sum(multi, axis=1, keepdims=True)
    run_ref[...] = run
    cnt_ref[...] = run.astype(I32)


def route(x2, norm_g, router_w, router_b, *, tm=ROUTE_TM):
    t, d = x2.shape
    tm = min(tm, t)
    ne = router_w.shape[1]
    tri = jnp.asarray(np.triu(np.ones((tm, tm), np.float32), k=1), BF16)
    lane4 = pl.BlockSpec((TOP_K, tm), lambda i: (0, i))
    return pl.pallas_call(
        _route_kernel,
        grid=(t // tm,),
        in_specs=[pl.BlockSpec((tm, d), lambda i: (i, 0)), pl.BlockSpec((1, d), lambda i: (0, 0)),
                  pl.BlockSpec((ne, d), lambda i: (0, 0)), pl.BlockSpec((ne, 1), lambda i: (0, 0)),
                  pl.BlockSpec((tm, tm), lambda i: (0, 0))],
        out_specs=[pl.BlockSpec((tm, d), lambda i: (i, 0)), lane4, lane4, lane4,
                   pl.BlockSpec((ne, LANES), lambda i: (0, 0))],
        out_shape=[jax.ShapeDtypeStruct((t, d), F32), jax.ShapeDtypeStruct((TOP_K, t), I32),
                   jax.ShapeDtypeStruct((TOP_K, t), F32), jax.ShapeDtypeStruct((TOP_K, t), I32),
                   jax.ShapeDtypeStruct((ne, LANES), I32)],
        scratch_shapes=[pltpu.VMEM((ne, LANES), F32)],
        compiler_params=_cparams(("arbitrary",)),
        name="route",
    )(x2, norm_g.reshape(1, d), router_w.T.astype(F32), router_b.reshape(ne, 1).astype(F32), tri)


def _pos_kernel(ps_ref, ti_ref, rank_ref, pos_ref):
    ti = ti_ref[...]
    acc = rank_ref[...]
    for e in range(N_EXPERTS):
        acc = acc + jnp.where(ti == e, ps_ref[e], 0)
    pos_ref[...] = acc


def sorted_positions(pad_start, top_i, rank, *, tm=2048):
    k, t = top_i.shape
    tm = min(tm, t)
    spec = pl.BlockSpec((k, tm), lambda i, *_: (0, i))
    return pl.pallas_call(
        _pos_kernel,
        grid_spec=pltpu.PrefetchScalarGridSpec(num_scalar_prefetch=1, grid=(t // tm,), in_specs=[spec, spec],
                                               out_specs=spec),
        out_shape=jax.ShapeDtypeStruct((k, t), I32),
        compiler_params=_cparams(("arbitrary",)),
        name="sorted_positions",
    )(pad_start, top_i, rank)


def _per_block(v, tm):
    k, t = v.shape
    return v.reshape(k, t // tm, tm).transpose(1, 0, 2).reshape(t // tm, k * tm)


def _dispatch_kernel(meta_ref, pos_hbm, h_ref, xs_hbm, pos_smem, zero_ref, sem_pos, sem_row, sem_zero):
    i = pl.program_id(0)
    tm = h_ref.shape[0]
    ne = N_EXPERTS
    bm = zero_ref.shape[0]

    def zero_copy(dst_row):
        return pltpu.make_async_copy(zero_ref.at[pl.ds(0, 1)], xs_hbm.at[pl.ds(dst_row, 1)], sem_zero)

    def zero_block_copy(blk):
        return pltpu.make_async_copy(zero_ref, xs_hbm.at[pl.ds(pl.multiple_of(blk * bm, bm), bm)], sem_zero)

    @pl.when(i == 0)
    def _():
        zero_ref[...] = jnp.zeros(zero_ref.shape, F32)
        first_unused = meta_ref[3 * ne] // bm

        def tail_start(blk, c):
            zero_block_copy(blk).start()
            return c

        def tail_wait(blk, c):
            zero_block_copy(blk).wait()
            return c

        lax.fori_loop(first_unused, xs_hbm.shape[0] // bm, tail_start, 0)
        lax.fori_loop(first_unused, xs_hbm.shape[0] // bm, tail_wait, 0)

        def per_expert(e, carry):
            first = meta_ref[e] + meta_ref[ne + e]
            n = meta_ref[2 * ne + e] - meta_ref[ne + e]

            def start(r, c):
                zero_copy(first + r).start()
                return c

            def wait(r, c):
                zero_copy(first + r).wait()
                return c

            lax.fori_loop(0, n, start, 0)
            lax.fori_loop(0, n, wait, 0)
            return carry

        lax.fori_loop(0, ne, per_expert, 0)

    pos_copy = pltpu.make_async_copy(pos_hbm.at[i], pos_smem, sem_pos)
    pos_copy.start()
    pos_copy.wait()

    def row_copy(t, k):
        return pltpu.make_async_copy(h_ref.at[pl.ds(t, 1)], xs_hbm.at[pl.ds(pos_smem[k * tm + t], 1)], sem_row)

    def start(t, c):
        for k in range(TOP_K):
            row_copy(t, k).start()
        return c

    def wait(t, c):
        for k in range(TOP_K):
            row_copy(t, k).wait()
        return c

    lax.fori_loop(0, tm, start, 0, unroll=8)
    lax.fori_loop(0, tm, wait, 0, unroll=8)


def dispatch(meta, pos, h, n_rows, bm, *, tm=DISPATCH_TM):
    t, d = h.shape
    tm = min(tm, t)
    return pl.pallas_call(
        _dispatch_kernel,
        grid_spec=pltpu.PrefetchScalarGridSpec(
            num_scalar_prefetch=1,
            grid=(t // tm,),
            in_specs=[pl.BlockSpec(memory_space=pl.ANY), pl.BlockSpec((tm, d), lambda i, *_: (i, 0))],
            out_specs=pl.BlockSpec(memory_space=pl.ANY),
            scratch_shapes=[pltpu.SMEM((TOP_K * tm,), I32), pltpu.VMEM((bm, d), F32),
                            pltpu.SemaphoreType.DMA, pltpu.SemaphoreType.DMA, pltpu.SemaphoreType.DMA],
        ),
        out_shape=jax.ShapeDtypeStruct((n_rows, d), F32),
        compiler_params=_cparams(("arbitrary",)),
        name="dispatch",
    )(meta, _per_block(pos, tm), h)


def _expert_kernel(be_ref, nu_ref, x_ref, w1_ref, b1_ref, w2_ref, b2_ref, y_ref, w1b_ref, w2b_ref):
    i = pl.program_id(0)
    f = w2_ref.shape[1]
    used = i < nu_ref[0]
    prev = be_ref[jnp.maximum(i - 1, 0)]
    fresh = (i == 0) | (be_ref[i] != prev)

    @pl.when(used & fresh)
    def _():
        w1b_ref[...] = w1_ref[0].astype(BF16)
        w2b_ref[...] = w2_ref[0].astype(BF16)

    @pl.when(used)
    def _():
        u = jnp.dot(x_ref[...].astype(BF16), w1b_ref[...], preferred_element_type=F32) + b1_ref[0]
        g = jnp.minimum(u[:, :f], SWIGLU_LIMIT)
        lin = jnp.clip(u[:, f:], -SWIGLU_LIMIT, SWIGLU_LIMIT)
        a = (g * _sigmoid(SWIGLU_ALPHA * g) * (lin + 1.0)).astype(BF16)
        y_ref[...] = jnp.dot(a, w2b_ref[...], preferred_element_type=F32) + b2_ref[0]

    @pl.when(jnp.logical_not(used))
    def _():
        y_ref[...] = jnp.zeros(y_ref.shape, F32)


def expert_matmul(blk_e, n_used, xs, w1, b1, w2, b2, *, bm=EXPERT_BM):
    n_rows, d = xs.shape
    ne, _, f2 = w1.shape
    f = w2.shape[1]

    def live(i, be, nu):
        return jnp.minimum(i, jnp.maximum(nu[0] - 1, 0))

    return pl.pallas_call(
        _expert_kernel,
        grid_spec=pltpu.PrefetchScalarGridSpec(
            num_scalar_prefetch=2,
            grid=(n_rows // bm,),
            in_specs=[
                pl.BlockSpec((bm, d), lambda i, be, nu: (live(i, be, nu), 0)),
                pl.BlockSpec((1, d, f2), lambda i, be, nu: (be[live(i, be, nu)], 0, 0)),
                pl.BlockSpec((1, 1, f2), lambda i, be, nu: (be[live(i, be, nu)], 0, 0)),
                pl.BlockSpec((1, f, d), lambda i, be, nu: (be[live(i, be, nu)], 0, 0)),
                pl.BlockSpec((1, 1, d), lambda i, be, nu: (be[live(i, be, nu)], 0, 0)),
            ],
            out_specs=pl.BlockSpec((bm, d), lambda i, be, nu: (i, 0)),
            scratch_shapes=[pltpu.VMEM((d, f2), BF16), pltpu.VMEM((f, d), BF16)],
        ),
        out_shape=jax.ShapeDtypeStruct((n_rows, d), F32),
        compiler_params=_cparams(("arbitrary",)),
        name="expert_matmul",
    )(blk_e, n_used, xs, w1, b1.reshape(ne, 1, f2), w2, b2.reshape(ne, 1, d))


def _combine_kernel(pos_hbm, ys_hbm, x_ref, gate_ref, fg_ref, o_ref, pos_smem, ybuf_ref, sem_pos, sem_row,
                    *, final_norm):
    i = pl.program_id(0)
    tm = x_ref.shape[0]
    pos_copy = pltpu.make_async_copy(pos_hbm.at[i], pos_smem, sem_pos)
    pos_copy.start()
    pos_copy.wait()

    def row_copy(t, k):
        return pltpu.make_async_copy(ys_hbm.at[pl.ds(pos_smem[k * tm + t], 1)], ybuf_ref.at[k, pl.ds(t, 1)], sem_row)

    def start(t, c):
        for k in range(TOP_K):
            row_copy(t, k).start()
        return c

    def wait(t, c):
        for k in range(TOP_K):
            row_copy(t, k).wait()
        return c

    lax.fori_loop(0, tm, start, 0, unroll=8)
    lax.fori_loop(0, tm, wait, 0, unroll=8)

    gates = gate_ref[...]
    acc = x_ref[...]
    for k in range(TOP_K):
        acc = acc + gates[:, k:k + 1] * ybuf_ref[k]
    if final_norm:
        acc = _rmsnorm(acc, fg_ref[...])
    o_ref[...] = acc


def combine(pos, ys, x2, gates, final_g, *, final_norm, tm=COMBINE_TM):
    t, d = x2.shape
    tm = min(tm, t)
    return pl.pallas_call(
        functools.partial(_combine_kernel, final_norm=final_norm),
        grid=(t // tm,),
        in_specs=[pl.BlockSpec(memory_space=pl.ANY), pl.BlockSpec(memory_space=pl.ANY),
                  pl.BlockSpec((tm, d), lambda i: (i, 0)), pl.BlockSpec((tm, TOP_K), lambda i: (i, 0)),
                  pl.BlockSpec((1, d), lambda i: (0, 0))],
        out_specs=pl.BlockSpec((tm, d), lambda i: (i, 0)),
        out_shape=jax.ShapeDtypeStruct((t, d), F32),
        scratch_shapes=[pltpu.SMEM((TOP_K * tm,), I32), pltpu.VMEM((TOP_K, tm, d), F32),
                        pltpu.SemaphoreType.DMA, pltpu.SemaphoreType.DMA],
        compiler_params=_cparams(("arbitrary",)),
        name="combine",
    )(_per_block(pos, tm), ys, x2, gates.T, final_g.reshape(1, d))


def moe_layer(x2, norm_g, router_w, router_b, w1, b1, w2, b2, final_g, *, final_norm, bm=EXPERT_BM):
    t, d = x2.shape
    ne = router_w.shape[1]
    h, top_i, gates, rank, cnt = route(x2, norm_g, router_w, router_b)
    counts = cnt[:, 0]
    padded = (counts + bm - 1) // bm * bm
    pad_end = jnp.cumsum(padded)
    pad_start = pad_end - padded
    n_rows = t * TOP_K + ne * bm
    n_blk = n_rows // bm
    blk_e = jnp.minimum(jnp.searchsorted(pad_end, jnp.arange(n_blk, dtype=I32) * bm, side="right"),
                        ne - 1).astype(I32)
    n_used = (pad_end[-1:] // bm).astype(I32)
    meta = jnp.concatenate([pad_start, counts, padded, pad_end[-1:]]).astype(I32)
    pos = sorted_positions(pad_start.astype(I32), top_i, rank)
    xs = dispatch(meta, pos, h, n_rows, bm)
    ys = expert_matmul(blk_e, n_used, xs, w1, b1, w2, b2, bm=bm)
    return combine(pos, ys, x2, gates, final_g, final_norm=final_norm)


def kernel(x, conv_norm_g, conv_pw1_w, conv_pw1_b, conv_dw_w, conv_dw_b, conv_ln_g, conv_ln_b, conv_pw2_w,
           conv_pw2_b, kv_norm_g, w_kv, b_kv, attn_norm_g, w_q, b_q, sinks, w_o, b_o, rel_bias, moe_norm_g,
           router_w, router_b, moe_w1, moe_b1, moe_w2, moe_b2, final_norm_g):
    b, s, d = x.shape
    depth = moe_norm_g.shape[0]
    n_conv = conv_norm_g.shape[0]
    bias = bias_table(rel_bias)
    k2 = v2 = None
    for l in range(depth):
        if l < n_conv:
            x = conv_mixer(x, conv_norm_g[l], conv_pw1_w[l], conv_pw1_b[l], conv_dw_w[l], conv_dw_b[l],
                           conv_ln_g[l], conv_ln_b[l], conv_pw2_w[l], conv_pw2_b[l])
        else:
            j = l - n_conv
            x = swa_attention(x, k2, v2, bias, attn_norm_g[j], w_q[j], b_q[j], sinks[j], w_o[j], b_o[j])
        x = moe_layer(x.reshape(b * s, d), moe_norm_g[l], router_w[l], router_b[l], moe_w1[l], moe_b1[l],
                      moe_w2[l], moe_b2[l], final_norm_g, final_norm=(l == depth - 1)).reshape(b, s, d)
        if l == n_conv - 1:
            k2, v2 = kv_project(x.reshape(b * s, d), kv_norm_g, w_kv, b_kv)
    return x
```
